```python
import math
import jax, jax.numpy as jnp
from jax import lax
import numpy as np

D_MODEL = 2048
BATCH = 1
SEQ = 8192
DEPTH = 4

HEAD_DIM = 128
GRID_W = 64
NA_HEADS = 8
NA_WIDTH = NA_HEADS * HEAD_DIM
NA_WIN_R = 8
NA_WIN_C = 16
WA_HEADS = 8
WA_KV_HEADS = 2
WA_WIDTH = WA_HEADS * HEAD_DIM
WA_KV_WIDTH = WA_KV_HEADS * HEAD_DIM
WA_WINDOW = 128
WA_BLOCK = 128
MIX_WIDTH = NA_WIDTH + WA_WIDTH
PROJ_SPLITS = (NA_WIDTH, NA_WIDTH, NA_WIDTH, WA_WIDTH, WA_KV_WIDTH, WA_KV_WIDTH)
PROJ_WIDTH = sum(PROJ_SPLITS)
D_FF = 5632
CONV_W = 3
ROPE_THETA = 10000.0
EPS = 1e-6
NEG = -1e30

kernel_name = "hybrid_na_swa_convffn_encoder"


def rms_norm(x, g):
    xf = x.astype(jnp.float32)
    y = xf * lax.rsqrt(jnp.mean(xf * xf, axis=-1, keepdims=True) + EPS)
    return (y * g.astype(jnp.float32)).astype(x.dtype)


def rope(x, positions):
    d = x.shape[-1]
    inv = ROPE_THETA ** (-jnp.arange(0, d, 2, dtype=jnp.float32) / d)
    ang = positions.astype(jnp.float32)[:, None] * inv[None, :]
    cos = jnp.cos(ang)[None, :, None, :]
    sin = jnp.sin(ang)[None, :, None, :]
    xf = x.astype(jnp.float32)
    x1, x2 = xf[..., : d // 2], xf[..., d // 2:]
    out = jnp.concatenate([x1 * cos - x2 * sin, x2 * cos + x1 * sin], axis=-1)
    return out.astype(x.dtype)


def neighborhood_attention(q, k, v, rpb):
    b, s, h, d = q.shape
    rows = s // GRID_W
    wr = min(NA_WIN_R, rows)
    wc = NA_WIN_C
    qg = q.reshape(b, rows, GRID_W, h, d)
    kg = k.reshape(b, rows, GRID_W, h, d)
    vg = v.reshape(b, rows, GRID_W, h, d)
    r = jnp.arange(rows)
    row_start = jnp.clip(r - wr // 2, 0, rows - wr)
    row_idx = row_start[:, None] + jnp.arange(wr)[None, :]
    k_rows = kg[:, row_idx]
    v_rows = vg[:, row_idx]
    c = jnp.arange(GRID_W)
    col_start = jnp.clip(c - wc // 2, 0, GRID_W - wc)
    col_mask = (c[None, :] >= col_start[:, None]) & (c[None, :] < col_start[:, None] + wc)
    dr = row_idx - r[:, None] + (NA_WIN_R - 1)
    dc = jnp.clip(c[None, :] - c[:, None], -(wc - 1), wc - 1) + (NA_WIN_C - 1)
    bias = rpb[:, dr]
    bias = bias[:, :, :, dc]
    bias = jnp.transpose(bias, (0, 1, 3, 2, 4)).astype(jnp.float32)
    scale = 1.0 / math.sqrt(d)
    sc = jnp.einsum('brqhd,brwkhd->bhrqwk', qg, k_rows).astype(jnp.float32) * scale
    sc = sc + bias[None]
    sc = jnp.where(col_mask[:, None, :], sc, NEG)
    shp = sc.shape
    p = jax.nn.softmax(sc.reshape(shp[:-2] + (wr * GRID_W,)), axis=-1).reshape(shp)
    out = jnp.einsum('bhrqwk,brwkhd->brqhd', p.astype(v.dtype), v_rows)
    return out.reshape(b, s, h * d)


def windowed_gqa_sink(q, k, v, sink):
    b, s, hq, d = q.shape
    hkv = k.shape[2]
    g = hq // hkv
    nb = s // WA_BLOCK
    qb = q.reshape(b, nb, WA_BLOCK, hkv, g, d)
    pad = ((0, 0), (WA_BLOCK, WA_BLOCK), (0, 0), (0, 0))
    kp = jnp.pad(k, pad).reshape(b, nb + 2, WA_BLOCK, hkv, d)
    vp = jnp.pad(v, pad).reshape(b, nb + 2, WA_BLOCK, hkv, d)
    kw = jnp.concatenate([kp[:, :-2], kp[:, 1:-1], kp[:, 2:]], axis=2)
    vw = jnp.concatenate([vp[:, :-2], vp[:, 1:-1], vp[:, 2:]], axis=2)
    blk = jnp.arange(nb)[:, None]
    qpos = blk * WA_BLOCK + jnp.arange(WA_BLOCK)[None, :]
    kpos = (blk - 1) * WA_BLOCK + jnp.arange(3 * WA_BLOCK)[None, :]
    diff = kpos[:, None, :] - qpos[:, :, None]
    valid = (jnp.abs(diff) <= WA_WINDOW) & (kpos[:, None, :] >= 0) & (kpos[:, None, :] < s)
    scale = 1.0 / math.sqrt(d)
    sc = jnp.einsum('bnqhgd,bnkhd->bhgnqk', qb, kw).astype(jnp.float32) * scale
    sc = jnp.where(valid, sc, NEG)
    sink_l = sink.astype(jnp.float32).reshape(hkv, g)[None, :, :, None, None, None]
    m = jnp.maximum(jnp.max(sc, axis=-1, keepdims=True), sink_l)
    e = jnp.exp(sc - m)
    p = e / (jnp.sum(e, axis=-1, keepdims=True) + jnp.exp(sink_l - m))
    out = jnp.einsum('bhgnqk,bnkhd->bnqhgd', p.astype(v.dtype), vw)
    return out.reshape(b, s, hq * d)


def depthwise_conv(u, w, bias):
    up = jnp.pad(u, ((0, 0), (1, 1), (0, 0)))
    return up[:, :-2] * w[0] + up[:, 1:-1] * w[1] + up[:, 2:] * w[2] + bias


def setup_inputs(seed: int = 0) -> dict:
    key = jax.random.key(seed)
    ks = jax.random.split(key, 20)
    f32 = jnp.float32
    nrm = lambda k, shp, sc: jax.random.normal(k, shp, f32) * sc
    centre = jnp.zeros((CONV_W, 1), f32).at[CONV_W // 2].set(1.0)
    return {
        "x": nrm(ks[0], (BATCH, SEQ, D_MODEL), 1.0),
        "positions": jnp.arange(SEQ, dtype=jnp.int32),
        "ln1_g": 1.0 + nrm(ks[1], (DEPTH, D_MODEL), 0.02),
        "w_in": nrm(ks[2], (DEPTH, D_MODEL, PROJ_WIDTH), D_MODEL ** -0.5),
        "qn_a": 1.0 + nrm(ks[3], (DEPTH, HEAD_DIM), 0.02),
        "kn_a": 1.0 + nrm(ks[4], (DEPTH, HEAD_DIM), 0.02),
        "rpb": nrm(ks[5], (DEPTH, NA_HEADS, 2 * NA_WIN_R - 1, 2 * NA_WIN_C - 1), 0.1),
        "qn_b": 1.0 + nrm(ks[6], (DEPTH, HEAD_DIM), 0.02),
        "kn_b": 1.0 + nrm(ks[7], (DEPTH, HEAD_DIM), 0.02),
        "sink": nrm(ks[8], (DEPTH, WA_HEADS), 0.5),
        "on_a": 1.0 + nrm(ks[9], (DEPTH, NA_WIDTH), 0.02),
        "on_b": 1.0 + nrm(ks[10], (DEPTH, WA_WIDTH), 0.02),
        "w_out": nrm(ks[11], (DEPTH, MIX_WIDTH, D_MODEL), 0.5 * MIX_WIDTH ** -0.5),
        "ln2_g": 1.0 + nrm(ks[12], (DEPTH, D_MODEL), 0.02),
        "w_up": nrm(ks[13], (DEPTH, D_MODEL, 2 * D_FF), D_MODEL ** -0.5),
        "conv_w": centre[None] + nrm(ks[14], (DEPTH, CONV_W, 2 * D_FF), 0.3),
        "conv_b": nrm(ks[15], (DEPTH, 2 * D_FF), 0.01),
        "w_down": nrm(ks[16], (DEPTH, D_FF, D_MODEL), 0.5 * D_FF ** -0.5),
    }


def reference(x, positions, ln1_g, w_in, qn_a, kn_a, rpb, qn_b, kn_b, sink,
              on_a, on_b, w_out, ln2_g, w_up, conv_w, conv_b, w_down):
    b, s, _ = x.shape
    cuts = list(np.cumsum(PROJ_SPLITS)[:-1])
    for l in range(DEPTH):
        h = rms_norm(x, ln1_g[l])
        proj = h @ w_in[l]
        qa, ka, va, qb, kb, vb = jnp.split(proj, cuts, axis=-1)
        qa = rms_norm(qa.reshape(b, s, NA_HEADS, HEAD_DIM), qn_a[l])
        ka = rms_norm(ka.reshape(b, s, NA_HEADS, HEAD_DIM), kn_a[l])
        va = va.reshape(b, s, NA_HEADS, HEAD_DIM)
        oa = neighborhood_attention(qa, ka, va, rpb[l])
        qb = rope(rms_norm(qb.reshape(b, s, WA_HEADS, HEAD_DIM), qn_b[l]), positions)
        kb = rope(rms_norm(kb.reshape(b, s, WA_KV_HEADS, HEAD_DIM), kn_b[l]), positions)
        vb = vb.reshape(b, s, WA_KV_HEADS, HEAD_DIM)
        ob = windowed_gqa_sink(qb, kb, vb, sink[l])
        o = jnp.concatenate([rms_norm(oa, on_a[l]), rms_norm(ob, on_b[l])], axis=-1)
        x = x + o @ w_out[l]
        h2 = rms_norm(x, ln2_g[l])
        u = depthwise_conv(h2 @ w_up[l], conv_w[l], conv_b[l])
        gate, up = u[..., :D_FF], u[..., D_FF:]
        x = x + (jax.nn.silu(gate) * up) @ w_down[l]
    return x
```

```python
import functools
import math

import jax
import jax.numpy as jnp
from jax import lax
from jax.experimental import pallas as pl
from jax.experimental.pallas import tpu as pltpu

HEAD_DIM = 128
GRID_W = 64
NA_HEADS = 8
NA_WIN_R = 8
NA_WIN_C = 16
WA_HEADS = 8
WA_KV_HEADS = 2
WA_WINDOW = 128
CONV_W = 3
ROPE_THETA = 10000.0
EPS = 1e-6
NEG = -1e30

LANES = 128
SUBLANES_F32 = 8
SUBLANES_BF16 = 16
VMEM_LIMIT_BYTES = 56 * 1024 * 1024

BF16 = jnp.bfloat16
F32 = jnp.float32

PROJ_TM = 1024
PROJ_TN = 256
OUT_TM = 1024
OUT_TN = 512
FFN_TM = 1024
FFN_TF = 256
FFN_HALO = SUBLANES_BF16
FFN_DOWN_TN = 512
NORM_CHUNK = 256
ROPE_TM = 1024
NA_BLOCK_ROWS = 8
NA_KEY_ROWS = 16
WA_TQ = 256


def _params(*sem):
    return pltpu.CompilerParams(dimension_semantics=sem, vmem_limit_bytes=VMEM_LIMIT_BYTES)


def _rms_rows(x_ref, g_ref, dst_ref, n_rows, dst_off, dst_col=0):
    g = g_ref[...]
    d = x_ref.shape[-1]
    chunk = min(NORM_CHUNK, n_rows)

    def body(c, carry):
        r = pl.multiple_of(c * chunk, chunk)
        x = x_ref[pl.ds(r, chunk), :]
        ms = jnp.mean(x * x, axis=-1, keepdims=True)
        y = x * lax.rsqrt(ms + EPS) * g
        dst_ref[pl.ds(dst_off + r, chunk), dst_col:dst_col + d] = y.astype(dst_ref.dtype)
        return carry

    lax.fori_loop(0, n_rows // chunk, body, 0)


def _rope_table_kernel(pos_ref, inv_ref, cos_ref, sin_ref):
    ang = pos_ref[...].astype(F32) * inv_ref[...]
    lane = lax.broadcasted_iota(jnp.int32, ang.shape, 1)
    s = jnp.sin(ang)
    cos_ref[...] = jnp.cos(ang)
    sin_ref[...] = jnp.where(lane < HEAD_DIM // 2, -s, s)


def _rope_tables(positions):
    s = positions.shape[0]
    inv = ROPE_THETA ** (-jnp.arange(0, HEAD_DIM, 2, dtype=F32) / HEAD_DIM)
    inv_full = jnp.concatenate([inv, inv])[None, :]
    tm = min(ROPE_TM, s)
    return pl.pallas_call(
        _rope_table_kernel,
        out_shape=(jax.ShapeDtypeStruct((s, HEAD_DIM), F32),) * 2,
        grid=(s // tm,),
        in_specs=[pl.BlockSpec((tm, 1), lambda i: (i, 0)),
                  pl.BlockSpec((1, HEAD_DIM), lambda i: (0, 0))],
        out_specs=(pl.BlockSpec((tm, HEAD_DIM), lambda i: (i, 0)),) * 2,
        compiler_params=_params("arbitrary"),
        name="rope_tables",
    )(positions[:, None], inv_full)


def _inproj_kernel(x_ref, g_ref, w_ref, gain_ref, cos_ref, sin_ref, o_ref, h_ref, *,
                   seg_ends, q_scale):
    j = pl.program_id(1)
    tm = x_ref.shape[0]
    qa_e, ka_e, va_e, qb_e, kb_e, _ = seg_ends

    @pl.when(j == 0)
    def _():
        _rms_rows(x_ref, g_ref, h_ref, tm, 0)

    acc = jnp.dot(h_ref[...], w_ref[...].astype(BF16), preferred_element_type=F32)
    n_heads = acc.shape[1] // HEAD_DIM
    is_q = (j < qa_e) | ((j >= va_e) & (j < qb_e))
    scale = jnp.where(is_q, q_scale, 1.0).astype(F32)
    gain = gain_ref[0]

    def normed(hh):
        a = acc[:, hh * HEAD_DIM:(hh + 1) * HEAD_DIM]
        ms = jnp.mean(a * a, axis=-1, keepdims=True)
        return a * lax.rsqrt(ms + EPS) * gain

    @pl.when((j >= ka_e) & (j < va_e) | (j >= kb_e))
    def _():
        o_ref[...] = acc.astype(o_ref.dtype)

    @pl.when(j < ka_e)
    def _():
        for hh in range(n_heads):
            o_ref[:, hh * HEAD_DIM:(hh + 1) * HEAD_DIM] = (normed(hh) * scale).astype(o_ref.dtype)

    @pl.when((j >= va_e) & (j < kb_e))
    def _():
        cos = cos_ref[...]
        sin = sin_ref[...]
        for hh in range(n_heads):
            y = normed(hh)
            y = y * cos + pltpu.roll(y, HEAD_DIM // 2, 1) * sin
            o_ref[:, hh * HEAD_DIM:(hh + 1) * HEAD_DIM] = (y * scale).astype(o_ref.dtype)


def _in_projection(x, ln_g, w_in, layer, gains, cos, sin, seg_ends):
    s, d = x.shape
    n = w_in.shape[2]
    tm = min(PROJ_TM, s)
    tn = PROJ_TN
    kern = functools.partial(_inproj_kernel, seg_ends=seg_ends, q_scale=1.0 / math.sqrt(HEAD_DIM))
    return pl.pallas_call(
        kern,
        out_shape=jax.ShapeDtypeStruct((s, n), BF16),
        grid=(s // tm, n // tn),
        in_specs=[pl.BlockSpec((tm, d), lambda i, j: (i, 0)),
                  pl.BlockSpec((1, d), lambda i, j: (0, 0)),
                  pl.BlockSpec((None, d, tn), lambda i, j: (layer, 0, j)),
                  pl.BlockSpec((1, 1, HEAD_DIM), lambda i, j: (j, 0, 0)),
                  pl.BlockSpec((tm, HEAD_DIM), lambda i, j: (i, 0)),
                  pl.BlockSpec((tm, HEAD_DIM), lambda i, j: (i, 0))],
        out_specs=pl.BlockSpec((tm, tn), lambda i, j: (i, j)),
        scratch_shapes=[pltpu.VMEM((tm, d), BF16)],
        compiler_params=_params("parallel", "arbitrary"),
        name="in_projection",
    )(x, ln_g, w_in, gains, cos, sin)


def _na_geometry(rows):
    n_blocks = rows // NA_BLOCK_ROWS
    out = []
    for b in (0, 1, n_blocks - 1):
        r0 = b * NA_BLOCK_ROWS
        k0 = min(max(r0 - NA_WIN_R // 2, 0), rows - NA_KEY_ROWS)
        out.append((r0, k0))
    return out


def _na_bias_kernel(u_ref, o_ref, *, rows, geometry):
    shape = (GRID_W, LANES)
    lane = lax.broadcasted_iota(jnp.int32, shape, 1)
    cq = lax.broadcasted_iota(jnp.int32, shape, 0)
    ck = lane & (GRID_W - 1)
    col_start = jnp.clip(cq - NA_WIN_C // 2, 0, GRID_W - NA_WIN_C)
    col_ok = (ck >= col_start) & (ck < col_start + NA_WIN_C)
    left = lane < GRID_W
    neg = jnp.full(shape, NEG, F32)

    n_pairs = u_ref.shape[1]
    tiles = []
    for p in range(n_pairs):
        u = jnp.broadcast_to(u_ref[0, p:p + 1, :], shape)
        t = pltpu.roll(u, 0, 1, stride=1, stride_axis=0)
        tiles.append(jnp.where(col_ok, t, neg))

    for v, (r0, k0) in enumerate(geometry):
        for qi in range(NA_BLOCK_ROWS):
            rq = r0 + qi
            row_start = min(max(rq - NA_WIN_R // 2, 0), rows - NA_WIN_R)
            for m in range(NA_KEY_ROWS // 2):
                rk0 = k0 + 2 * m
                ok0 = row_start <= rk0 < row_start + NA_WIN_R
                ok1 = row_start <= rk0 + 1 < row_start + NA_WIN_R
                dr0 = rk0 - rq + (NA_WIN_R - 1)
                if ok0 and ok1:
                    tile = tiles[dr0 + 1]
                elif ok0:
                    tile = jnp.where(left, tiles[dr0 + 1], neg)
                elif ok1:
                    tile = jnp.where(left, neg, tiles[dr0 + 1])
                else:
                    tile = neg
                o_ref[0, v, qi * GRID_W:(qi + 1) * GRID_W, m * LANES:(m + 1) * LANES] = tile


def _na_bias(rpb, rows):
    h, n_dr, n_dc = rpb.shape
    half = n_dc // 2
    z = jnp.zeros((h, 1, n_dc), F32)
    t_lo = jnp.concatenate([z, rpb], axis=1)
    t_hi = jnp.concatenate([rpb, z], axis=1)
    zeros = lambda w: jnp.zeros((h, n_dr + 1, w), F32)
    gap = GRID_W - n_dc
    u = jnp.concatenate([t_lo[..., half:], zeros(gap), t_hi, zeros(gap), t_lo[..., :half]], axis=-1)
    assert u.shape[-1] == LANES
    geometry = _na_geometry(rows)
    nq = NA_BLOCK_ROWS * GRID_W
    nk = NA_KEY_ROWS * GRID_W
    kern = functools.partial(_na_bias_kernel, rows=rows, geometry=geometry)
    return pl.pallas_call(
        kern,
        out_shape=jax.ShapeDtypeStruct((h, 3, nq, nk), F32),
        grid=(h,),
        in_specs=[pl.BlockSpec((1, n_dr + 1, LANES), lambda i: (i, 0, 0))],
        out_specs=pl.BlockSpec((1, 3, nq, nk), lambda i: (i, 0, 0, 0)),
        compiler_params=_params("arbitrary"),
        name="na_bias",
    )(u)


def _na_attn_kernel(q_ref, k_ref, v_ref, b_ref, o_ref, *, rows):
    b = pl.program_id(1)
    nk = NA_KEY_ROWS * GRID_W
    k_row0 = jnp.clip(b * NA_BLOCK_ROWS - NA_WIN_R // 2, 0, rows - NA_KEY_ROWS)
    start = pl.multiple_of(k_row0 * GRID_W, (NA_WIN_R // 2) * GRID_W)
    k = k_ref[pl.ds(start, nk), :]
    v = v_ref[pl.ds(start, nk), :]
    sc = lax.dot_general(q_ref[...], k, (((1,), (1,)), ((), ())), preferred_element_type=F32)
    sc = sc + b_ref[0, 0]
    m = jnp.max(sc, axis=-1, keepdims=True)
    e = jnp.exp(sc - m)
    l = jnp.sum(e, axis=-1, keepdims=True)
    o = jnp.dot(e.astype(BF16), v, preferred_element_type=F32)
    o_ref[...] = o / l


def _na_attention(proj, bias, q_col, k_col, v_col):
    s = proj.shape[0]
    rows = s // GRID_W
    nq = NA_BLOCK_ROWS * GRID_W
    n_blocks = rows // NA_BLOCK_ROWS
    nk = NA_KEY_ROWS * GRID_W

    def variant(b):
        return jnp.where(b == 0, 0, jnp.where(b == n_blocks - 1, 2, 1))

    kern = functools.partial(_na_attn_kernel, rows=rows)
    return pl.pallas_call(
        kern,
        out_shape=jax.ShapeDtypeStruct((s, NA_HEADS * HEAD_DIM), F32),
        grid=(NA_HEADS, n_blocks),
        in_specs=[pl.BlockSpec((nq, HEAD_DIM), lambda h, b: (b, q_col + h)),
                  pl.BlockSpec((s, HEAD_DIM), lambda h, b: (0, k_col + h)),
                  pl.BlockSpec((s, HEAD_DIM), lambda h, b: (0, v_col + h)),
                  pl.BlockSpec((1, 1, nq, nk), lambda h, b: (h, variant(b), 0, 0))],
        out_specs=pl.BlockSpec((nq, HEAD_DIM), lambda h, b: (b, h)),
        compiler_params=_params("parallel", "arbitrary"),
        name="na_attention",
    )(proj, proj, proj, bias)


def _wa_attn_kernel(sink_ref, q_ref, k_ref, v_ref, o_ref, *, seq):
    g = pl.program_id(0)
    n = pl.program_id(1)
    tq = q_ref.shape[0]
    nk = tq + 2 * WA_WINDOW
    group = q_ref.shape[1] // HEAD_DIM
    start = pl.multiple_of(jnp.clip(n * tq - WA_WINDOW, 0, seq - nk), WA_WINDOW)
    k = k_ref[pl.ds(start, nk), :]
    v = v_ref[pl.ds(start, nk), :]
    qpos = n * tq + lax.broadcasted_iota(jnp.int32, (tq, nk), 0)
    kpos = start + lax.broadcasted_iota(jnp.int32, (tq, nk), 1)
    valid = jnp.abs(kpos - qpos) <= WA_WINDOW
    for hh in range(group):
        sink = sink_ref[g * group + hh]
        q = q_ref[:, hh * HEAD_DIM:(hh + 1) * HEAD_DIM]
        sc = lax.dot_general(q, k, (((1,), (1,)), ((), ())), preferred_element_type=F32)
        sc = jnp.where(valid, sc, NEG)
        m = jnp.maximum(jnp.max(sc, axis=-1, keepdims=True), sink)
        e = jnp.exp(sc - m)
        l = jnp.sum(e, axis=-1, keepdims=True) + jnp.exp(sink - m)
        o = jnp.dot(e.astype(BF16), v, preferred_element_type=F32)
        o_ref[:, hh * HEAD_DIM:(hh + 1) * HEAD_DIM] = o / l


def _wa_attention(proj, sink, q_col, k_col, v_col):
    s = proj.shape[0]
    group = WA_HEADS // WA_KV_HEADS
    gw = group * HEAD_DIM
    tq = min(WA_TQ, s - 2 * WA_WINDOW)
    kern = functools.partial(_wa_attn_kernel, seq=s)
    return pl.pallas_call(
        kern,
        out_shape=jax.ShapeDtypeStruct((s, WA_HEADS * HEAD_DIM), F32),
        grid=(WA_KV_HEADS, s // tq),
        in_specs=[pl.BlockSpec(memory_space=pltpu.SMEM),
                  pl.BlockSpec((tq, gw), lambda g, n: (n, q_col // group + g)),
                  pl.BlockSpec((s, HEAD_DIM), lambda g, n: (0, k_col + g)),
                  pl.BlockSpec((s, HEAD_DIM), lambda g, n: (0, v_col + g))],
        out_specs=pl.BlockSpec((tq, gw), lambda g, n: (n, g)),
        compiler_params=_params("parallel", "arbitrary"),
        name="wa_attention",
    )(sink, proj, proj, proj)


def _outproj_kernel(oa_ref, ob_ref, ga_ref, gb_ref, w_ref, x_ref, o_ref, n_ref):
    j = pl.program_id(1)
    tm = oa_ref.shape[0]
    wa = oa_ref.shape[1]

    @pl.when(j == 0)
    def _():
        _rms_rows(oa_ref, ga_ref, n_ref, tm, 0, 0)
        _rms_rows(ob_ref, gb_ref, n_ref, tm, 0, wa)

    y = jnp.dot(n_ref[...], w_ref[...].astype(BF16), preferred_element_type=F32)
    o_ref[...] = x_ref[...] + y


def _out_projection(oa, ob, on_a, on_b, w_out, layer, x):
    s, wa = oa.shape
    wb = ob.shape[1]
    d = w_out.shape[2]
    tm = min(OUT_TM, s)
    tn = OUT_TN
    return pl.pallas_call(
        _outproj_kernel,
        out_shape=jax.ShapeDtypeStruct((s, d), F32),
        grid=(s // tm, d // tn),
        in_specs=[pl.BlockSpec((tm, wa), lambda i, j: (i, 0)),
                  pl.BlockSpec((tm, wb), lambda i, j: (i, 0)),
                  pl.BlockSpec((1, wa), lambda i, j: (0, 0)),
                  pl.BlockSpec((1, wb), lambda i, j: (0, 0)),
                  pl.BlockSpec((None, wa + wb, tn), lambda i, j: (layer, 0, j)),
                  pl.BlockSpec((tm, tn), lambda i, j: (i, j))],
        out_specs=pl.BlockSpec((tm, tn), lambda i, j: (i, j)),
        scratch_shapes=[pltpu.VMEM((tm, wa + wb), BF16)],
        compiler_params=_params("parallel", "arbitrary"),
        name="out_projection",
    )(oa, ob, on_a, on_b, w_out, x)


def _ffn_kernel(xm_ref, xp_ref, xn_ref, g_ref, wg_ref, wu_ref, cwg_ref, cwu_ref, cbg_ref, cbu_ref,
                wd_ref, o_ref, h_ref):
    i = pl.program_id(0)
    j = pl.program_id(1)
    n_i = pl.num_programs(0)
    tm = xm_ref.shape[0]
    halo = xp_ref.shape[0]
    ext = tm + 2 * halo

    @pl.when(j == 0)
    def _():
        _rms_rows(xp_ref, g_ref, h_ref, halo, 0)
        _rms_rows(xn_ref, g_ref, h_ref, halo, halo + tm)
        _rms_rows(xm_ref, g_ref, h_ref, tm, halo)
        o_ref[...] = xm_ref[...]

        @pl.when(i == 0)
        def _():
            h_ref[0:halo, :] = jnp.zeros((halo, h_ref.shape[1]), h_ref.dtype)

        @pl.when(i == n_i - 1)
        def _():
            h_ref[halo + tm:ext, :] = jnp.zeros((halo, h_ref.shape[1]), h_ref.dtype)

    h = h_ref[...]

    def conv_branch(w_ref, cw_ref, cb_ref):
        u = jnp.dot(h, w_ref[...].astype(BF16), preferred_element_type=F32)
        u_prev = pltpu.roll(u, 1, 0)[halo:halo + tm]
        u_next = pltpu.roll(u, ext - 1, 0)[halo:halo + tm]
        cw = cw_ref[...]
        return u_prev * cw[0:1] + u[halo:halo + tm] * cw[1:2] + u_next * cw[2:3] + cb_ref[...]

    gate = conv_branch(wg_ref, cwg_ref, cbg_ref)
    up = conv_branch(wu_ref, cwu_ref, cbu_ref)
    act = (gate * (1.0 / (1.0 + jnp.exp(-gate))) * up).astype(BF16)
    d = o_ref.shape[1]
    tn = min(FFN_DOWN_TN, d)
    for c in range(d // tn):
        wd = wd_ref[:, c * tn:(c + 1) * tn].astype(BF16)
        o_ref[:, c * tn:(c + 1) * tn] += jnp.dot(act, wd, preferred_element_type=F32)


def _conv_ffn(x, ln_g, w_up, conv_w, conv_b, w_down, layer):
    s, d = x.shape
    d_ff = w_down.shape[1]
    tm = min(FFN_TM, s)
    tf = FFN_TF
    halo = FFN_HALO
    n_f = d_ff // tf
    per_tile = tm // halo
    last_halo_block = s // halo - 1
    return pl.pallas_call(
        _ffn_kernel,
        out_shape=jax.ShapeDtypeStruct((s, d), F32),
        grid=(s // tm, n_f),
        in_specs=[pl.BlockSpec((tm, d), lambda i, j: (i, 0)),
                  pl.BlockSpec((halo, d), lambda i, j: (jnp.maximum(i * per_tile - 1, 0), 0)),
                  pl.BlockSpec((halo, d),
                               lambda i, j: (jnp.minimum((i + 1) * per_tile, last_halo_block), 0)),
                  pl.BlockSpec((1, d), lambda i, j: (0, 0)),
                  pl.BlockSpec((None, d, tf), lambda i, j: (layer, 0, j)),
                  pl.BlockSpec((None, d, tf), lambda i, j: (layer, 0, n_f + j)),
                  pl.BlockSpec((CONV_W, tf), lambda i, j: (0, j)),
                  pl.BlockSpec((CONV_W, tf), lambda i, j: (0, n_f + j)),
                  pl.BlockSpec((1, tf), lambda i, j: (0, j)),
                  pl.BlockSpec((1, tf), lambda i, j: (0, n_f + j)),
                  pl.BlockSpec((None, tf, d), lambda i, j: (layer, j, 0))],
        out_specs=pl.BlockSpec((tm, d), lambda i, j: (i, 0)),
        scratch_shapes=[pltpu.VMEM((tm + 2 * halo, d), BF16)],
        compiler_params=_params("parallel", "arbitrary"),
        name="conv_ffn",
    )(x, x, x, ln_g, w_up, w_up, conv_w, conv_w, conv_b, conv_b, w_down)


def kernel(x, positions, ln1_g, w_in, qn_a, kn_a, rpb, qn_b, kn_b, sink, on_a, on_b, w_out, ln2_g,
           w_up, conv_w, conv_b, w_down):
    b, s, d = x.shape
    depth = w_in.shape[0]
    na_w = NA_HEADS * HEAD_DIM
    wa_w = WA_HEADS * HEAD_DIM
    kv_w = WA_KV_HEADS * HEAD_DIM
    seg_widths = (na_w, na_w, na_w, wa_w, kv_w, kv_w)
    seg_ends = tuple(int(e) // PROJ_TN for e in
                     [sum(seg_widths[:k + 1]) for k in range(len(seg_widths))])
    seg_chunks = tuple(w // PROJ_TN for w in seg_widths)
    col = lambda k: sum(seg_widths[:k]) // HEAD_DIM
    rows = s // GRID_W

    assert b == 1, "the token grid kernels take one sequence"
    cos, sin = _rope_tables(positions)
    xb = x.reshape(s, d)
    ones = jnp.ones((HEAD_DIM,), F32)
    for l in range(depth):
        seg_gain = (qn_a[l], kn_a[l], ones, qn_b[l], kn_b[l], ones)
        gains = jnp.concatenate(
            [jnp.broadcast_to(gv[None, None, :], (n, 1, HEAD_DIM))
             for gv, n in zip(seg_gain, seg_chunks)], axis=0)
        proj = _in_projection(xb, ln1_g[l][None, :], w_in, l, gains, cos, sin, seg_ends)
        bias = _na_bias(rpb[l], rows)
        oa = _na_attention(proj, bias, col(0), col(1), col(2))
        ob = _wa_attention(proj, sink[l], col(3), col(4), col(5))
        xb = _out_projection(oa, ob, on_a[l][None, :], on_b[l][None, :], w_out, l, xb)
        xb = _conv_ffn(xb, ln2_g[l][None, :], w_up, conv_w[l], conv_b[l][None, :], w_down, l)
    return xb.reshape(b, s, d)
```
